```python
import jax, jax.numpy as jnp
from jax import lax
import numpy as np

D_MODEL = 2048
BATCH = 1
SEQ = 8192
DEPTH = 2
DEC_BATCH = 32
DEC_SEQ = 32
PAST_LEN = 1024

CHUNK = 64
D_A = 2048
CONV_A = 3
D_B = 2048
CONV_B = 31
D_C = 2048
CHUNK_C = 128
N_GROUPS_C = 16
GROUP_C = D_C // N_GROUPS_C
N_BRANCH = 3
D_FF = ((8 * D_MODEL + 3 * 256 - 1) // (3 * 256)) * 256
ALPHA = (2 * DEPTH) ** 0.25
BETA = (8 * DEPTH) ** -0.25
LN_EPS = 1e-5
N_IN = 3 * D_A + 2 * D_B + 2 * D_C + N_BRANCH * D_MODEL
SPLITS = (D_A, 2 * D_A, 3 * D_A, 3 * D_A + D_B, 3 * D_A + 2 * D_B,
          3 * D_A + 2 * D_B + D_C, 3 * D_A + 2 * D_B + 2 * D_C)

kernel_name = "hybrid_streaming_conv_gmlp_encoder_step"


def _layernorm(x, g, b):
    xf = x.astype(jnp.float32)
    mu = xf.mean(-1, keepdims=True)
    var = jnp.square(xf - mu).mean(-1, keepdims=True)
    return ((xf - mu) * lax.rsqrt(var + LN_EPS) * g + b).astype(x.dtype)


def _causal_dwconv(hist, x, w):
    xp = jnp.concatenate([hist.astype(x.dtype), x], axis=1)
    y = lax.conv_general_dilated(xp, w[:, None, :].astype(x.dtype), window_strides=(1,),
                                 padding='VALID', dimension_numbers=('NWC', 'WIO', 'NWC'),
                                 feature_group_count=x.shape[-1])
    return y, xp[:, -(w.shape[0] - 1):]


def _spatial_mix(v, ws, bs):
    b, t, _ = v.shape
    n = -(-t // CHUNK_C)
    vp = jnp.pad(v, ((0, 0), (0, n * CHUNK_C - t), (0, 0))).reshape(b, n, CHUNK_C, N_GROUPS_C, GROUP_C)
    mask = jnp.tril(jnp.ones((CHUNK_C, CHUNK_C), dtype=bool))
    wm = jnp.where(mask, ws, 0).astype(v.dtype)
    s = jnp.einsum('gij,bnjgc->bnigc', wm, vp) + bs.T.astype(v.dtype)[None, None, :, :, None]
    return s.reshape(b, n * CHUNK_C, D_C)[:, :t]


def _layer(x, hist_a, hist_b, w_in, b_gate, conv_a_w, conv_b_w, conv_b_b, ln_b_g, ln_b_b,
           ln_v_g, ln_v_b, w_s, b_s, w_a_out, w_b_out, w_c_out, w_o, ln1_g, ln1_b,
           w_gate, w_up, w_down, ln2_g, ln2_b):
    proj = x @ w_in
    h, bg, cg, a, ga, u, v, gl = jnp.split(proj, SPLITS, axis=-1)
    ya, new_a = _causal_dwconv(hist_a, cg * h, conv_a_w)
    p_a = (bg * ya) @ w_a_out
    yb, new_b = _causal_dwconv(hist_b, a * jax.nn.sigmoid(ga), conv_b_w)
    p_b = jax.nn.silu(_layernorm(yb + conv_b_b, ln_b_g, ln_b_b)) @ w_b_out
    vn = _layernorm(jax.nn.gelu(v), ln_v_g, ln_v_b)
    p_c = (jax.nn.gelu(u) * _spatial_mix(vn, w_s, b_s)) @ w_c_out
    g = jax.nn.sigmoid(gl + b_gate).reshape(gl.shape[:-1] + (N_BRANCH, D_MODEL))
    m = g[..., 0, :] * p_a + g[..., 1, :] * p_b + g[..., 2, :] * p_c
    x = _layernorm(ALPHA * x + m @ w_o, ln1_g, ln1_b)
    f = (jax.nn.silu(x @ w_gate) * (x @ w_up)) @ w_down
    x = _layernorm(ALPHA * x + f, ln2_g, ln2_b)
    return x, new_a, new_b, vn


def setup_inputs(seed: int = 0) -> dict:
    key = jax.random.key(seed)
    ks = jax.random.split(key, 32)
    f32 = jnp.float32
    nrm = lambda k, s, sc: jax.random.normal(k, s, f32) * sc
    L, D = DEPTH, D_MODEL
    return {
        "x_prompt": nrm(ks[0], (BATCH, SEQ, D), 1.0),
        "x_sample": nrm(ks[1], (DEC_BATCH, DEC_SEQ, D), 1.0),
        "state_conv_a": nrm(ks[2], (L, DEC_BATCH, CONV_A - 1, D_A), 1.0),
        "state_conv_b": nrm(ks[3], (L, DEC_BATCH, CONV_B - 1, D_B), 0.5),
        "w_in": nrm(ks[4], (L, D, N_IN), D ** -0.5),
        "b_gate": nrm(ks[5], (L, N_BRANCH * D), 0.01),
        "conv_a_w": nrm(ks[6], (L, CONV_A, D_A), CONV_A ** -0.5),
        "conv_b_w": nrm(ks[7], (L, CONV_B, D_B), CONV_B ** -0.5),
        "conv_b_b": nrm(ks[8], (L, D_B), 0.01),
        "ln_b_g": 1.0 + nrm(ks[9], (L, D_B), 0.01),
        "ln_b_b": nrm(ks[10], (L, D_B), 0.01),
        "ln_v_g": 1.0 + nrm(ks[11], (L, D_C), 0.01),
        "ln_v_b": nrm(ks[12], (L, D_C), 0.01),
        "w_s": nrm(ks[13], (L, N_GROUPS_C, CHUNK_C, CHUNK_C), 0.5 * CHUNK_C ** -0.5),
        "b_s": 1.0 + nrm(ks[14], (L, N_GROUPS_C, CHUNK_C), 0.01),
        "w_a_out": nrm(ks[15], (L, D_A, D), BETA * D_A ** -0.5),
        "w_b_out": nrm(ks[16], (L, D_B, D), BETA * D_B ** -0.5),
        "w_c_out": nrm(ks[17], (L, D_C, D), BETA * D_C ** -0.5),
        "w_o": nrm(ks[18], (L, D, D), BETA * D ** -0.5),
        "ln1_g": 1.0 + nrm(ks[19], (L, D), 0.01),
        "ln1_b": nrm(ks[20], (L, D), 0.01),
        "w_gate": nrm(ks[21], (L, D, D_FF), D ** -0.5),
        "w_up": nrm(ks[22], (L, D, D_FF), D ** -0.5),
        "w_down": nrm(ks[23], (L, D_FF, D), BETA * D_FF ** -0.5),
        "ln2_g": 1.0 + nrm(ks[24], (L, D), 0.01),
        "ln2_b": nrm(ks[25], (L, D), 0.01),
    }


def reference(x_prompt, x_sample, state_conv_a, state_conv_b, w_in, b_gate, conv_a_w, conv_b_w,
              conv_b_b, ln_b_g, ln_b_b, ln_v_g, ln_v_b, w_s, b_s, w_a_out, w_b_out, w_c_out,
              w_o, ln1_g, ln1_b, w_gate, w_up, w_down, ln2_g, ln2_b):
    assert x_sample.shape[1] <= CHUNK
    hp, hs = x_prompt, x_sample
    zero_a = jnp.zeros((x_prompt.shape[0], CONV_A - 1, D_A), x_prompt.dtype)
    zero_b = jnp.zeros((x_prompt.shape[0], CONV_B - 1, D_B), x_prompt.dtype)
    pa, pb, sa, sb, sv = [], [], [], [], []
    for l in range(DEPTH):
        params = (w_in[l], b_gate[l], conv_a_w[l], conv_b_w[l], conv_b_b[l], ln_b_g[l], ln_b_b[l],
                  ln_v_g[l], ln_v_b[l], w_s[l], b_s[l], w_a_out[l], w_b_out[l], w_c_out[l],
                  w_o[l], ln1_g[l], ln1_b[l], w_gate[l], w_up[l], w_down[l], ln2_g[l], ln2_b[l])
        hp, na, nb, _ = _layer(hp, zero_a, zero_b, *params)
        hs, ma, mb, vs = _layer(hs, state_conv_a[l], state_conv_b[l], *params)
        pa.append(na); pb.append(nb); sa.append(ma); sb.append(mb); sv.append(vs)
    return (hp, hs, jnp.stack(pa), jnp.stack(pb), jnp.stack(sa), jnp.stack(sb), jnp.stack(sv))
```

```python
import functools

import jax
import jax.numpy as jnp
from jax import lax
from jax.experimental import pallas as pl
from jax.experimental.pallas import tpu as pltpu

D_MODEL = 2048
SEQ = 8192
DEPTH = 2
DEC_BATCH = 32
DEC_SEQ = 32
D_A = 2048
CONV_A = 3
D_B = 2048
CONV_B = 31
D_C = 2048
CHUNK_C = 128
N_GROUPS_C = 16
GROUP_C = D_C // N_GROUPS_C
N_BRANCH = 3
D_FF = 5632
ALPHA = (2 * DEPTH) ** 0.25
LN_EPS = 1e-5
N_PROJ_GROUPS = 7 + N_BRANCH

T_SAMPLE = DEC_BATCH * DEC_SEQ
T_ALL = SEQ + T_SAMPLE

V7X_VMEM_BYTES = 64 * 1024 * 1024
SUBLANES = 8
LANES = 128

F32 = jnp.float32
BF16 = jnp.bfloat16

PROJ_TM = 512
PROJ_TC = 256
MIX_TM = 256
SUB = DEC_SEQ
HALO_A = SUBLANES
HALO_B = SUB
CONV_LANES = 256
MERGE_TM = 512
MERGE_TN = 512
OUT_TM = 512
FFN_TM = 512
FFN_TF = 512

N_PROMPT_MIX_TILES = SEQ // MIX_TM
SEQS_PER_MIX_TILE = MIX_TM // DEC_SEQ
SUBS_PER_MIX_TILE = MIX_TM // SUB


def _vmem_limit(nbytes):
    return int(min(V7X_VMEM_BYTES - 4 * 1024 * 1024, nbytes))


def _layernorm_rows(x, g, b):
    mu = jnp.mean(x, axis=-1, keepdims=True)
    xc = x - mu
    var = jnp.mean(xc * xc, axis=-1, keepdims=True)
    return xc * lax.rsqrt(var + LN_EPS) * g + b


def _proj_kernel(x_ref, w_ref, bgate_ref, ca_ref, bg_ref, ab_ref, gu_ref, gv_ref,
                 g0_ref, g1_ref, g2_ref):
    x = x_ref[...]
    tc = PROJ_TC

    def col(g):
        return jnp.dot(x, w_ref[0, :, g * tc:(g + 1) * tc], preferred_element_type=F32)

    ca_ref[...] = col(2) * col(0)
    bg_ref[...] = col(1)
    ab_ref[...] = col(3) * jax.nn.sigmoid(col(4))
    gu_ref[...] = jax.nn.gelu(col(5))
    gv_ref[...] = jax.nn.gelu(col(6))
    for k, ref in enumerate((g0_ref, g1_ref, g2_ref)):
        ref[...] = jax.nn.sigmoid(col(7 + k) + bgate_ref[k:k + 1, :])


def _proj_call(xb, w_tiles, bgate):
    n_c = D_MODEL // PROJ_TC
    n_m = T_ALL // PROJ_TM
    out_spec = pl.BlockSpec((PROJ_TM, PROJ_TC), lambda j, i: (i, j))
    out_shape = jax.ShapeDtypeStruct((T_ALL, D_MODEL), F32)
    return pl.pallas_call(
        _proj_kernel,
        grid=(n_c, n_m),
        in_specs=[
            pl.BlockSpec((PROJ_TM, D_MODEL), lambda j, i: (i, 0)),
            pl.BlockSpec((1, D_MODEL, N_PROJ_GROUPS * PROJ_TC), lambda j, i: (j, 0, 0)),
            pl.BlockSpec((N_BRANCH, PROJ_TC), lambda j, i: (0, j)),
        ],
        out_specs=[out_spec] * 8,
        out_shape=[out_shape] * 8,
        compiler_params=pltpu.CompilerParams(
            dimension_semantics=("arbitrary", "arbitrary"),
            vmem_limit_bytes=_vmem_limit(56 * 1024 * 1024)),
        name="proj",
    )(xb, w_tiles, bgate)


def _mix_kernel(ca_ref, bg_ref, ab_ref, gu_ref, gv_ref, ha_ref, hb_ref,
                wa_ref, wb_ref, bb_ref, lnbg_ref, lnbb_ref, lnvg_ref, lnvb_ref,
                ws_ref, bs_ref,
                za_ref, zb_ref, zc_ref, vn_ref,
                wina_ref, winb_ref, ya_ref, yb_ref, carry_a_ref, carry_b_ref):
    i = pl.program_id(0)
    is_prompt = i < N_PROMPT_MIX_TILES

    @pl.when(i == 0)
    def _():
        carry_a_ref[...] = jnp.zeros_like(carry_a_ref)
        carry_b_ref[...] = jnp.zeros_like(carry_b_ref)

    for r in range(SUBS_PER_MIX_TILE):
        wina_ref[r, HALO_A:, :] = ca_ref[r * SUB:(r + 1) * SUB, :]
        winb_ref[r, HALO_B:, :] = ab_ref[r * SUB:(r + 1) * SUB, :]

    @pl.when(is_prompt)
    def _():
        wina_ref[0, :HALO_A, :] = carry_a_ref[...]
        winb_ref[0, :HALO_B, :] = carry_b_ref[...]
        for r in range(1, SUBS_PER_MIX_TILE):
            wina_ref[r, :HALO_A, :] = ca_ref[r * SUB - HALO_A:r * SUB, :]
            winb_ref[r, :HALO_B, :] = ab_ref[r * SUB - HALO_B:r * SUB, :]
        carry_a_ref[...] = ca_ref[MIX_TM - HALO_A:, :]
        carry_b_ref[...] = ab_ref[MIX_TM - HALO_B:, :]

    @pl.when(jnp.logical_not(is_prompt))
    def _():
        for r in range(SUBS_PER_MIX_TILE):
            wina_ref[r, :HALO_A, :] = ha_ref[r]
            winb_ref[r, :HALO_B, :] = hb_ref[r]

    n_lane_chunks = D_MODEL // CONV_LANES

    def conv_step(idx, carry):
        r = idx // n_lane_chunks
        c = pl.multiple_of((idx % n_lane_chunks) * CONV_LANES, CONV_LANES)
        lanes = pl.ds(c, CONV_LANES)
        rows = pl.ds(pl.multiple_of(r * SUB, SUB), SUB)
        acc = None
        for k in range(CONV_A):
            off = HALO_A - (CONV_A - 1) + k
            term = wa_ref[k:k + 1, lanes] * wina_ref[r, off:off + SUB, lanes]
            acc = term if acc is None else acc + term
        ya_ref[rows, lanes] = acc
        acc = None
        for k in range(CONV_B):
            off = HALO_B - (CONV_B - 1) + k
            term = wb_ref[k:k + 1, lanes] * winb_ref[r, off:off + SUB, lanes]
            acc = term if acc is None else acc + term
        yb_ref[rows, lanes] = acc + bb_ref[:, lanes]
        return carry

    lax.fori_loop(0, SUBS_PER_MIX_TILE * n_lane_chunks, conv_step, 0)

    def row_step(q, carry):
        rows = pl.ds(pl.multiple_of(q * SUB, SUB), SUB)
        za_ref[rows, :] = (bg_ref[rows, :] * ya_ref[rows, :]).astype(BF16)
        nb = _layernorm_rows(yb_ref[rows, :], lnbg_ref[...], lnbb_ref[...])
        zb_ref[rows, :] = (nb * jax.nn.sigmoid(nb)).astype(BF16)
        vn_ref[rows, :] = _layernorm_rows(gv_ref[rows, :], lnvg_ref[...], lnvb_ref[...])
        return carry

    lax.fori_loop(0, MIX_TM // SUB, row_step, 0)

    ri = lax.broadcasted_iota(jnp.int32, (CHUNK_C, CHUNK_C), 0)
    ci = lax.broadcasted_iota(jnp.int32, (CHUNK_C, CHUNK_C), 1)
    span_log2 = jnp.where(is_prompt, CHUNK_C.bit_length() - 1, DEC_SEQ.bit_length() - 1)
    span_low = jnp.where(is_prompt, CHUNK_C - 1, DEC_SEQ - 1)
    mask = jnp.logical_and((ri >> span_log2) == (ci >> span_log2),
                           (ci & span_low) <= (ri & span_low))
    for g in range(N_GROUPS_C):
        lanes = slice(g * GROUP_C, (g + 1) * GROUP_C)
        wm = jnp.where(mask, ws_ref[0, g], 0.0).astype(BF16)
        for q in range(MIX_TM // CHUNK_C):
            rows = slice(q * CHUNK_C, (q + 1) * CHUNK_C)
            s = jnp.dot(wm, vn_ref[rows, lanes].astype(BF16), preferred_element_type=F32)
            s = s + bs_ref[0, :, lanes]
            zc_ref[rows, lanes] = (gu_ref[rows, lanes] * s).astype(BF16)


def _mix_call(ca, bg, ab, gu, gv, halo_a, halo_b, conv_a_w, conv_b_w, conv_b_b,
              ln_b_g, ln_b_b, ln_v_g, ln_v_b, ws_stack, bs_stack):
    n_tiles = T_ALL // MIX_TM
    row_spec = pl.BlockSpec((MIX_TM, D_MODEL), lambda i: (i, 0))

    def sample_block(i):
        return jnp.maximum(i - N_PROMPT_MIX_TILES, 0)

    def which(i):
        return jnp.where(i < N_PROMPT_MIX_TILES, 0, 1)

    vec_spec = pl.BlockSpec((1, D_MODEL), lambda i: (0, 0))
    scratch = [
        pltpu.VMEM((SUBS_PER_MIX_TILE, HALO_A + SUB, D_MODEL), F32),
        pltpu.VMEM((SUBS_PER_MIX_TILE, HALO_B + SUB, D_MODEL), F32),
        pltpu.VMEM((MIX_TM, D_MODEL), F32),
        pltpu.VMEM((MIX_TM, D_MODEL), F32),
        pltpu.VMEM((HALO_A, D_MODEL), F32),
        pltpu.VMEM((HALO_B, D_MODEL), F32),
    ]
    return pl.pallas_call(
        _mix_kernel,
        grid=(n_tiles,),
        in_specs=[
            row_spec, row_spec, row_spec, row_spec, row_spec,
            pl.BlockSpec((SEQS_PER_MIX_TILE, HALO_A, D_MODEL), lambda i: (sample_block(i), 0, 0)),
            pl.BlockSpec((SEQS_PER_MIX_TILE, HALO_B, D_MODEL), lambda i: (sample_block(i), 0, 0)),
            pl.BlockSpec((CONV_A, D_MODEL), lambda i: (0, 0)),
            pl.BlockSpec((CONV_B, D_MODEL), lambda i: (0, 0)),
            vec_spec, vec_spec, vec_spec, vec_spec, vec_spec,
            pl.BlockSpec((1, N_GROUPS_C, CHUNK_C, CHUNK_C), lambda i: (which(i), 0, 0, 0)),
            pl.BlockSpec((1, CHUNK_C, D_MODEL), lambda i: (which(i), 0, 0)),
        ],
        out_specs=[row_spec, row_spec, row_spec, row_spec],
        out_shape=[
            jax.ShapeDtypeStruct((T_ALL, D_MODEL), BF16),
            jax.ShapeDtypeStruct((T_ALL, D_MODEL), BF16),
            jax.ShapeDtypeStruct((T_ALL, D_MODEL), BF16),
            jax.ShapeDtypeStruct((T_ALL, D_MODEL), F32),
        ],
        scratch_shapes=scratch,
        compiler_params=pltpu.CompilerParams(
            dimension_semantics=("arbitrary",),
            vmem_limit_bytes=_vmem_limit(56 * 1024 * 1024)),
        name="mix",
    )(ca, bg, ab, gu, gv, halo_a, halo_b, conv_a_w, conv_b_w, conv_b_b,
      ln_b_g, ln_b_b, ln_v_g, ln_v_b, ws_stack, bs_stack)


def _merge_kernel(za_ref, zb_ref, zc_ref, wa_ref, wb_ref, wc_ref,
                  g0_ref, g1_ref, g2_ref, m_ref):
    pa = jnp.dot(za_ref[...], wa_ref[...], preferred_element_type=F32)
    m = g0_ref[...] * pa
    pb = jnp.dot(zb_ref[...], wb_ref[...], preferred_element_type=F32)
    m = m + g1_ref[...] * pb
    pc = jnp.dot(zc_ref[...], wc_ref[...], preferred_element_type=F32)
    m = m + g2_ref[...] * pc
    m_ref[...] = m.astype(BF16)


def _merge_call(za, zb, zc, wa, wb, wc, g0, g1, g2):
    n_n = D_MODEL // MERGE_TN
    n_m = T_ALL // MERGE_TM
    z_spec = pl.BlockSpec((MERGE_TM, D_MODEL), lambda j, i: (i, 0))
    w_spec = pl.BlockSpec((D_MODEL, MERGE_TN), lambda j, i: (0, j))
    t_spec = pl.BlockSpec((MERGE_TM, MERGE_TN), lambda j, i: (i, j))
    return pl.pallas_call(
        _merge_kernel,
        grid=(n_n, n_m),
        in_specs=[z_spec, z_spec, z_spec, w_spec, w_spec, w_spec, t_spec, t_spec, t_spec],
        out_specs=t_spec,
        out_shape=jax.ShapeDtypeStruct((T_ALL, D_MODEL), BF16),
        compiler_params=pltpu.CompilerParams(
            dimension_semantics=("arbitrary", "arbitrary"),
            vmem_limit_bytes=_vmem_limit(48 * 1024 * 1024)),
        name="merge",
    )(za, zb, zc, wa, wb, wc, g0, g1, g2)


def _outp_kernel(m_ref, wo_ref, x_ref, g_ref, b_ref, x1_ref, x1b_ref):
    y = ALPHA * x_ref[...] + jnp.dot(m_ref[...], wo_ref[...], preferred_element_type=F32)
    x1 = _layernorm_rows(y, g_ref[...], b_ref[...])
    x1_ref[...] = x1
    x1b_ref[...] = x1.astype(BF16)


def _outp_call(m, wo, x, g, b):
    row_spec = pl.BlockSpec((OUT_TM, D_MODEL), lambda i: (i, 0))
    vec_spec = pl.BlockSpec((1, D_MODEL), lambda i: (0, 0))
    return pl.pallas_call(
        _outp_kernel,
        grid=(T_ALL // OUT_TM,),
        in_specs=[row_spec, pl.BlockSpec((D_MODEL, D_MODEL), lambda i: (0, 0)), row_spec,
                  vec_spec, vec_spec],
        out_specs=[row_spec, row_spec],
        out_shape=[jax.ShapeDtypeStruct((T_ALL, D_MODEL), F32),
                   jax.ShapeDtypeStruct((T_ALL, D_MODEL), BF16)],
        compiler_params=pltpu.CompilerParams(
            dimension_semantics=("arbitrary",),
            vmem_limit_bytes=_vmem_limit(48 * 1024 * 1024)),
        name="outp",
    )(m, wo, x, g, b)


def _ffn_kernel(xb_ref, wg_ref, wu_ref, wd_ref, x_ref, g_ref, b_ref,
                x2_ref, x2b_ref, acc_ref):
    k = pl.program_id(1)
    xb = xb_ref[...]
    hg = jnp.dot(xb, wg_ref[...], preferred_element_type=F32)
    hu = jnp.dot(xb, wu_ref[...], preferred_element_type=F32)
    h = (hg * jax.nn.sigmoid(hg) * hu).astype(BF16)
    part = jnp.dot(h, wd_ref[...], preferred_element_type=F32)

    @pl.when(k == 0)
    def _():
        acc_ref[...] = part

    @pl.when(k > 0)
    def _():
        acc_ref[...] += part

    @pl.when(k == pl.num_programs(1) - 1)
    def _():
        y = ALPHA * x_ref[...] + acc_ref[...]
        x2 = _layernorm_rows(y, g_ref[...], b_ref[...])
        x2_ref[...] = x2
        x2b_ref[...] = x2.astype(BF16)


def _ffn_call(x1b, wg, wu, wd, x1, g, b):
    row_spec = pl.BlockSpec((FFN_TM, D_MODEL), lambda i, k: (i, 0))
    vec_spec = pl.BlockSpec((1, D_MODEL), lambda i, k: (0, 0))
    return pl.pallas_call(
        _ffn_kernel,
        grid=(T_ALL // FFN_TM, D_FF // FFN_TF),
        in_specs=[row_spec,
                  pl.BlockSpec((D_MODEL, FFN_TF), lambda i, k: (0, k)),
                  pl.BlockSpec((D_MODEL, FFN_TF), lambda i, k: (0, k)),
                  pl.BlockSpec((FFN_TF, D_MODEL), lambda i, k: (k, 0)),
                  row_spec, vec_spec, vec_spec],
        out_specs=[row_spec, row_spec],
        out_shape=[jax.ShapeDtypeStruct((T_ALL, D_MODEL), F32),
                   jax.ShapeDtypeStruct((T_ALL, D_MODEL), BF16)],
        scratch_shapes=[pltpu.VMEM((FFN_TM, D_MODEL), F32)],
        compiler_params=pltpu.CompilerParams(
            dimension_semantics=("arbitrary", "arbitrary"),
            vmem_limit_bytes=_vmem_limit(48 * 1024 * 1024)),
        name="ffn",
    )(x1b, wg, wu, wd, x1, g, b)


def _row(v):
    return v.reshape(1, -1)


def kernel(x_prompt, x_sample, state_conv_a, state_conv_b, w_in, b_gate, conv_a_w, conv_b_w,
           conv_b_b, ln_b_g, ln_b_b, ln_v_g, ln_v_b, w_s, b_s, w_a_out, w_b_out, w_c_out,
           w_o, ln1_g, ln1_b, w_gate, w_up, w_down, ln2_g, ln2_b):
    x = jnp.concatenate([x_prompt.reshape(SEQ, D_MODEL), x_sample.reshape(T_SAMPLE, D_MODEL)], axis=0)
    xb = x.astype(BF16)
    n_c = D_MODEL // PROJ_TC
    reps = CHUNK_C // DEC_SEQ
    pa, pb, sa, sb, sv = [], [], [], [], []
    for l in range(DEPTH):
        w_tiles = (w_in[l].reshape(D_MODEL, N_PROJ_GROUPS, n_c, PROJ_TC)
                   .transpose(2, 0, 1, 3).reshape(n_c, D_MODEL, N_PROJ_GROUPS * PROJ_TC).astype(BF16))
        bgate = b_gate[l].reshape(N_BRANCH, D_MODEL)
        halo_a = jnp.pad(state_conv_a[l], ((0, 0), (HALO_A - (CONV_A - 1), 0), (0, 0)))
        halo_b = jnp.pad(state_conv_b[l], ((0, 0), (HALO_B - (CONV_B - 1), 0), (0, 0)))
        ws_stack = jnp.stack([w_s[l], jnp.tile(w_s[l][:, :DEC_SEQ, :DEC_SEQ], (1, reps, reps))])
        bs_rows = jnp.repeat(b_s[l].T, GROUP_C, axis=1)
        bs_stack = jnp.stack([bs_rows, jnp.tile(bs_rows[:DEC_SEQ], (reps, 1))])

        ca, bg, ab, gu, gv, g0, g1, g2 = _proj_call(xb, w_tiles, bgate)
        za, zb, zc, vn = _mix_call(ca, bg, ab, gu, gv, halo_a, halo_b, conv_a_w[l], conv_b_w[l],
                                   _row(conv_b_b[l]), _row(ln_b_g[l]), _row(ln_b_b[l]),
                                   _row(ln_v_g[l]), _row(ln_v_b[l]), ws_stack, bs_stack)
        m = _merge_call(za, zb, zc, w_a_out[l].astype(BF16), w_b_out[l].astype(BF16),
                        w_c_out[l].astype(BF16), g0, g1, g2)
        x1, x1b = _outp_call(m, w_o[l].astype(BF16), x, _row(ln1_g[l]), _row(ln1_b[l]))
        x, xb = _ffn_call(x1b, w_gate[l].astype(BF16), w_up[l].astype(BF16), w_down[l].astype(BF16),
                          x1, _row(ln2_g[l]), _row(ln2_b[l]))

        pa.append(ca[SEQ - (CONV_A - 1):SEQ][None])
        pb.append(ab[SEQ - (CONV_B - 1):SEQ][None])
        ca_s = ca[SEQ:].reshape(DEC_BATCH, DEC_SEQ, D_A)
        ab_s = ab[SEQ:].reshape(DEC_BATCH, DEC_SEQ, D_B)
        sa.append(ca_s[:, DEC_SEQ - (CONV_A - 1):])
        sb.append(ab_s[:, DEC_SEQ - (CONV_B - 1):])
        sv.append(vn[SEQ:].reshape(DEC_BATCH, DEC_SEQ, D_C))
    y_prompt = x[:SEQ].reshape(1, SEQ, D_MODEL)
    y_sample = x[SEQ:].reshape(DEC_BATCH, DEC_SEQ, D_MODEL)
    return (y_prompt, y_sample, jnp.stack(pa), jnp.stack(pb), jnp.stack(sa), jnp.stack(sb),
            jnp.stack(sv))
```

```python
import jax
import jax.numpy as jnp
from jax import lax
from jax.experimental import pallas as pl
from jax.experimental.pallas import tpu as pltpu

D_MODEL = 2048
SEQ = 8192
DEPTH = 2
DEC_BATCH = 32
DEC_SEQ = 32
D_A = 2048
CONV_A = 3
D_B = 2048
CONV_B = 31
D_C = 2048
CHUNK_C = 128
N_GROUPS_C = 16
GROUP_C = D_C // N_GROUPS_C
N_BRANCH = 3
D_FF = 5632
ALPHA = (2 * DEPTH) ** 0.25
LN_EPS = 1e-5
N_PROJ_GROUPS = 7 + N_BRANCH

T_SAMPLE = DEC_BATCH * DEC_SEQ
T_ALL = SEQ + T_SAMPLE

V7X_VMEM_BYTES = 64 * 1024 * 1024
SUBLANES = 8
LANES = 128
MIB = 1024 * 1024

F32 = jnp.float32
BF16 = jnp.bfloat16

PROJ_TM = 1024
PROJ_TC = LANES
CAST_STEPS = 4
MIX_TM = 256
SUB = DEC_SEQ
HALO_A = SUBLANES
HALO_B = SUB
CONV_LANES = 256
MERGE_TM = 512
MERGE_TN = 512
OUT_TM = 512
FFN_TM = 512
FFN_TF = 512
FFN_TN = 512
OUT_ROWS = 256

N_PROMPT_MIX_TILES = SEQ // MIX_TM
SEQS_PER_MIX_TILE = MIX_TM // DEC_SEQ
SUBS_PER_MIX_TILE = MIX_TM // SUB
N_CONV_LANE_CHUNKS = D_MODEL // CONV_LANES


def _layernorm_rows(x, g, b):
    mu = jnp.mean(x, axis=-1, keepdims=True)
    xc = x - mu
    var = jnp.mean(xc * xc, axis=-1, keepdims=True)
    return xc * lax.rsqrt(var + LN_EPS) * g + b


def _proj_kernel(*refs):
    x_ref = refs[0]
    w_refs = refs[1:1 + N_PROJ_GROUPS]
    bgate_ref = refs[1 + N_PROJ_GROUPS]
    n_cast = 7
    cast_in = refs[2 + N_PROJ_GROUPS:2 + N_PROJ_GROUPS + n_cast]
    outs = refs[2 + N_PROJ_GROUPS + n_cast:]
    ca_ref, bg_ref, ab_ref, gu_ref, gv_ref, g0_ref, g1_ref, g2_ref = outs[:8]
    cast_out = outs[8:8 + n_cast]
    wsc_ref = outs[8 + n_cast]
    tc = PROJ_TC
    i = pl.program_id(1)

    @pl.when(i == 0)
    def _():
        for g in range(N_PROJ_GROUPS):
            wsc_ref[:, g * tc:(g + 1) * tc] = w_refs[g][0].astype(BF16)

    @pl.when(i < CAST_STEPS)
    def _():
        for src, dst in zip(cast_in, cast_out):
            dst[...] = src[0].astype(BF16)

    x = x_ref[...]

    def pair(p):
        return jnp.dot(x, wsc_ref[:, 2 * p * tc:(2 * p + 2) * tc], preferred_element_type=F32)

    r4 = pair(4)
    g1_ref[...] = jax.nn.sigmoid(r4[:, :tc] + bgate_ref[1:2, :])
    g2_ref[...] = jax.nn.sigmoid(r4[:, tc:] + bgate_ref[2:3, :])
    r3 = pair(3)
    gv_ref[...] = jax.nn.gelu(r3[:, :tc])
    g0_ref[...] = jax.nn.sigmoid(r3[:, tc:] + bgate_ref[0:1, :])
    r2 = pair(2)
    gu_ref[...] = jax.nn.gelu(r2[:, tc:])
    sg = jax.nn.sigmoid(r2[:, :tc])
    r1 = pair(1)
    ab_ref[...] = r1[:, tc:] * sg
    r0 = pair(0)
    ca_ref[...] = r1[:, :tc] * r0[:, :tc]
    bg_ref[...] = r0[:, tc:]


def _proj_call(l, xb, w_in, bgate, cast_weights):
    n_c = D_MODEL // PROJ_TC
    n_m = T_ALL // PROJ_TM

    def cast_idx(i):
        return jnp.minimum(i, CAST_STEPS - 1)

    w_specs = [pl.BlockSpec((1, D_MODEL, PROJ_TC), lambda j, i, g=g: (l, 0, g * n_c + j))
               for g in range(N_PROJ_GROUPS)]
    cast_in_specs, cast_out_specs, cast_out_shapes = [], [], []
    for w in cast_weights:
        rows, cols = w.shape[1] // n_c, w.shape[2] // CAST_STEPS
        cast_in_specs.append(pl.BlockSpec((1, rows, cols), lambda j, i: (l, j, cast_idx(i))))
        cast_out_specs.append(pl.BlockSpec((rows, cols), lambda j, i: (j, cast_idx(i))))
        cast_out_shapes.append(jax.ShapeDtypeStruct(w.shape[1:], BF16))
    out_spec = pl.BlockSpec((PROJ_TM, PROJ_TC), lambda j, i: (i, j))
    out_shape = jax.ShapeDtypeStruct((T_ALL, D_MODEL), F32)
    res = pl.pallas_call(
        _proj_kernel,
        grid=(n_c, n_m),
        in_specs=[pl.BlockSpec((PROJ_TM, D_MODEL), lambda j, i: (i, 0))] + w_specs
                 + [pl.BlockSpec((N_BRANCH, PROJ_TC), lambda j, i: (0, j))] + cast_in_specs,
        out_specs=[out_spec] * 8 + cast_out_specs,
        out_shape=[out_shape] * 8 + cast_out_shapes,
        scratch_shapes=[pltpu.VMEM((D_MODEL, N_PROJ_GROUPS * PROJ_TC), BF16)],
        compiler_params=pltpu.CompilerParams(
            dimension_semantics=("arbitrary", "arbitrary"),
            vmem_limit_bytes=58 * MIB),
        name="proj",
    )(xb, *([w_in] * N_PROJ_GROUPS), bgate, *cast_weights)
    return res[:8], res[8:]


def _causal_taps(blocks, w_ref, lanes, n_taps, halo):
    first = halo - (n_taps - 1)
    y = None
    for rho in range(SUBLANES):
        taps = [k for k in range(n_taps) if (first + k) % SUBLANES == rho]
        if not taps:
            continue
        n_out = SUB // SUBLANES + (1 if rho else 0)
        acc = [None] * n_out
        for k in taps:
            q = (first + k) // SUBLANES
            w8 = w_ref[k, :, lanes]
            for b in range(n_out):
                term = w8 * blocks[q + b]
                acc[b] = term if acc[b] is None else acc[b] + term
        part = jnp.concatenate(acc, axis=0)
        if rho:
            part = part[rho:rho + SUB]
        y = part if y is None else y + part
    return y


def _mix_kernel(ca_ref, bg_ref, ab_ref, gu_ref, gv_ref, ha_ref, hb_ref,
                wa_ref, wb_ref, bb_ref, lnbg_ref, lnbb_ref, lnvg_ref, lnvb_ref,
                ws_ref, bs_ref,
                za_ref, zb_ref, zc_ref, vn_ref,
                ya_ref, yb_ref, carry_a_ref, carry_b_ref):
    i = pl.program_id(0)
    is_prompt = i < N_PROMPT_MIX_TILES
    n_a = (HALO_A + SUB) // SUBLANES
    n_b = (HALO_B + SUB) // SUBLANES

    @pl.when(i == 0)
    def _():
        carry_a_ref[...] = jnp.zeros_like(carry_a_ref)
        carry_b_ref[...] = jnp.zeros_like(carry_b_ref)

    def lane_slice(c):
        return pl.ds(pl.multiple_of(c * CONV_LANES, CONV_LANES), CONV_LANES)

    def emit(r, lanes, blocks_a, blocks_b):
        rows = pl.ds(pl.multiple_of(r * SUB, SUB), SUB)
        ya_ref[rows, lanes] = _causal_taps(blocks_a, wa_ref, lanes, CONV_A, HALO_A)
        yb_ref[rows, lanes] = _causal_taps(blocks_b, wb_ref, lanes, CONV_B, HALO_B) + bb_ref[:, lanes]

    def cur_blocks(ref, r, lanes, n):
        return [ref[pl.ds(pl.multiple_of(r * SUB + b * SUBLANES, SUBLANES), SUBLANES), lanes]
                for b in range(n)]

    @pl.when(is_prompt)
    def _():
        def first_step(c, carry):
            lanes = lane_slice(c)
            blocks_a = [carry_a_ref[:, lanes]] + cur_blocks(ca_ref, 0, lanes, SUB // SUBLANES)
            blocks_b = ([carry_b_ref[b * SUBLANES:(b + 1) * SUBLANES, lanes]
                         for b in range(HALO_B // SUBLANES)]
                        + cur_blocks(ab_ref, 0, lanes, SUB // SUBLANES))
            emit(0, lanes, blocks_a, blocks_b)
            return carry

        lax.fori_loop(0, N_CONV_LANE_CHUNKS, first_step, 0)

        def step(idx, carry):
            r = idx // N_CONV_LANE_CHUNKS
            lanes = lane_slice(idx % N_CONV_LANE_CHUNKS)

            def above(ref, halo, n):
                return [ref[pl.ds(pl.multiple_of(r * SUB - halo + b * SUBLANES, SUBLANES), SUBLANES), lanes]
                        for b in range(n)]

            emit(r, lanes, above(ca_ref, HALO_A, n_a), above(ab_ref, HALO_B, n_b))
            return carry

        lax.fori_loop(N_CONV_LANE_CHUNKS, SUBS_PER_MIX_TILE * N_CONV_LANE_CHUNKS, step, 0)
        carry_a_ref[...] = ca_ref[MIX_TM - HALO_A:, :]
        carry_b_ref[...] = ab_ref[MIX_TM - HALO_B:, :]

    @pl.when(jnp.logical_not(is_prompt))
    def _():
        def step(idx, carry):
            r = idx // N_CONV_LANE_CHUNKS
            lanes = lane_slice(idx % N_CONV_LANE_CHUNKS)
            blocks_a = [ha_ref[r, :, lanes]] + cur_blocks(ca_ref, r, lanes, SUB // SUBLANES)
            blocks_b = ([hb_ref[r, b * SUBLANES:(b + 1) * SUBLANES, lanes]
                         for b in range(HALO_B // SUBLANES)]
                        + cur_blocks(ab_ref, r, lanes, SUB // SUBLANES))
            emit(r, lanes, blocks_a, blocks_b)
            return carry

        lax.fori_loop(0, SUBS_PER_MIX_TILE * N_CONV_LANE_CHUNKS, step, 0)

    def row_step(q, carry):
        rows = pl.ds(pl.multiple_of(q * SUB, SUB), SUB)
        za_ref[rows, :] = (bg_ref[rows, :] * ya_ref[rows, :]).astype(BF16)
        nb = _layernorm_rows(yb_ref[rows, :], lnbg_ref[...], lnbb_ref[...])
        zb_ref[rows, :] = (nb * jax.nn.sigmoid(nb)).astype(BF16)
        vn_ref[rows, :] = _layernorm_rows(gv_ref[rows, :], lnvg_ref[...], lnvb_ref[...])
        return carry

    lax.fori_loop(0, MIX_TM // SUB, row_step, 0)

    ri = lax.broadcasted_iota(jnp.int32, (CHUNK_C, CHUNK_C), 0)
    ci = lax.broadcasted_iota(jnp.int32, (CHUNK_C, CHUNK_C), 1)
    span_log2 = jnp.where(is_prompt, CHUNK_C.bit_length() - 1, DEC_SEQ.bit_length() - 1)
    span_low = jnp.where(is_prompt, CHUNK_C - 1, DEC_SEQ - 1)
    mask = jnp.logical_and((ri >> span_log2) == (ci >> span_log2),
                           (ci & span_low) <= (ri & span_low))
    for g in range(N_GROUPS_C):
        lanes = slice(g * GROUP_C, (g + 1) * GROUP_C)
        wm = jnp.where(mask, ws_ref[0, g], 0.0).astype(BF16)
        for q in range(MIX_TM // CHUNK_C):
            rows = slice(q * CHUNK_C, (q + 1) * CHUNK_C)
            s = jnp.dot(wm, vn_ref[rows, lanes].astype(BF16), preferred_element_type=F32)
            s = s + bs_ref[0, :, lanes]
            zc_ref[rows, lanes] = (gu_ref[rows, lanes] * s).astype(BF16)


def _mix_call(ca, bg, ab, gu, gv, halo_a, halo_b, conv_a_w8, conv_b_w8, conv_b_b,
              ln_b_g, ln_b_b, ln_v_g, ln_v_b, ws_stack, bs_stack):
    n_tiles = T_ALL // MIX_TM
    row_spec = pl.BlockSpec((MIX_TM, D_MODEL), lambda i: (i, 0))

    def sample_block(i):
        return jnp.maximum(i - N_PROMPT_MIX_TILES, 0)

    def which(i):
        return jnp.where(i < N_PROMPT_MIX_TILES, 0, 1)

    vec_spec = pl.BlockSpec((1, D_MODEL), lambda i: (0, 0))
    scratch = [
        pltpu.VMEM((MIX_TM, D_MODEL), F32),
        pltpu.VMEM((MIX_TM, D_MODEL), F32),
        pltpu.VMEM((HALO_A, D_MODEL), F32),
        pltpu.VMEM((HALO_B, D_MODEL), F32),
    ]
    return pl.pallas_call(
        _mix_kernel,
        grid=(n_tiles,),
        in_specs=[
            row_spec, row_spec, row_spec, row_spec, row_spec,
            pl.BlockSpec((SEQS_PER_MIX_TILE, HALO_A, D_MODEL), lambda i: (sample_block(i), 0, 0)),
            pl.BlockSpec((SEQS_PER_MIX_TILE, HALO_B, D_MODEL), lambda i: (sample_block(i), 0, 0)),
            pl.BlockSpec((CONV_A, SUBLANES, D_MODEL), lambda i: (0, 0, 0)),
            pl.BlockSpec((CONV_B, SUBLANES, D_MODEL), lambda i: (0, 0, 0)),
            vec_spec, vec_spec, vec_spec, vec_spec, vec_spec,
            pl.BlockSpec((1, N_GROUPS_C, CHUNK_C, CHUNK_C), lambda i: (which(i), 0, 0, 0)),
            pl.BlockSpec((1, CHUNK_C, D_MODEL), lambda i: (which(i), 0, 0)),
        ],
        out_specs=[row_spec, row_spec, row_spec, row_spec],
        out_shape=[
            jax.ShapeDtypeStruct((T_ALL, D_MODEL), BF16),
            jax.ShapeDtypeStruct((T_ALL, D_MODEL), BF16),
            jax.ShapeDtypeStruct((T_ALL, D_MODEL), BF16),
            jax.ShapeDtypeStruct((T_ALL, D_MODEL), F32),
        ],
        scratch_shapes=scratch,
        compiler_params=pltpu.CompilerParams(
            dimension_semantics=("arbitrary",),
            vmem_limit_bytes=48 * MIB),
        name="mix",
    )(ca, bg, ab, gu, gv, halo_a, halo_b, conv_a_w8, conv_b_w8, conv_b_b,
      ln_b_g, ln_b_b, ln_v_g, ln_v_b, ws_stack, bs_stack)


def _merge_kernel(za_ref, zb_ref, zc_ref, wa_ref, wb_ref, wc_ref,
                  g0_ref, g1_ref, g2_ref, m_ref):
    pa = jnp.dot(za_ref[...], wa_ref[...], preferred_element_type=F32)
    m = g0_ref[...] * pa
    pb = jnp.dot(zb_ref[...], wb_ref[...], preferred_element_type=F32)
    m = m + g1_ref[...] * pb
    pc = jnp.dot(zc_ref[...], wc_ref[...], preferred_element_type=F32)
    m = m + g2_ref[...] * pc
    m_ref[...] = m.astype(BF16)


def _merge_call(za, zb, zc, wa, wb, wc, g0, g1, g2):
    n_n = D_MODEL // MERGE_TN
    n_m = T_ALL // MERGE_TM
    z_spec = pl.BlockSpec((MERGE_TM, D_MODEL), lambda j, i: (i, 0))
    w_spec = pl.BlockSpec((D_MODEL, MERGE_TN), lambda j, i: (0, j))
    t_spec = pl.BlockSpec((MERGE_TM, MERGE_TN), lambda j, i: (i, j))
    return pl.pallas_call(
        _merge_kernel,
        grid=(n_n, n_m),
        in_specs=[z_spec, z_spec, z_spec, w_spec, w_spec, w_spec, t_spec, t_spec, t_spec],
        out_specs=t_spec,
        out_shape=jax.ShapeDtypeStruct((T_ALL, D_MODEL), BF16),
        compiler_params=pltpu.CompilerParams(
            dimension_semantics=("arbitrary", "arbitrary"),
            vmem_limit_bytes=48 * MIB),
        name="merge",
    )(za, zb, zc, wa, wb, wc, g0, g1, g2)


def _outp_kernel(m_ref, wo_ref, x_ref, g_ref, b_ref, x1_ref, x1b_ref):
    for c in range(OUT_TM // OUT_ROWS):
        rows = slice(c * OUT_ROWS, (c + 1) * OUT_ROWS)
        y = ALPHA * x_ref[rows, :] + jnp.dot(m_ref[rows, :], wo_ref[...], preferred_element_type=F32)
        x1 = _layernorm_rows(y, g_ref[...], b_ref[...])
        x1_ref[rows, :] = x1
        x1b_ref[rows, :] = x1.astype(BF16)


def _outp_call(m, wo, x, g, b):
    row_spec = pl.BlockSpec((OUT_TM, D_MODEL), lambda i: (i, 0))
    vec_spec = pl.BlockSpec((1, D_MODEL), lambda i: (0, 0))
    return pl.pallas_call(
        _outp_kernel,
        grid=(T_ALL // OUT_TM,),
        in_specs=[row_spec, pl.BlockSpec((D_MODEL, D_MODEL), lambda i: (0, 0)), row_spec,
                  vec_spec, vec_spec],
        out_specs=[row_spec, row_spec],
        out_shape=[jax.ShapeDtypeStruct((T_ALL, D_MODEL), F32),
                   jax.ShapeDtypeStruct((T_ALL, D_MODEL), BF16)],
        compiler_params=pltpu.CompilerParams(
            dimension_semantics=("arbitrary",),
            vmem_limit_bytes=48 * MIB),
        name="outp",
    )(m, wo, x, g, b)


def _ffn_kernel(xb_ref, wg_ref, wu_ref, wd_ref, x_ref, g_ref, b_ref,
                x2_ref, x2b_ref, acc_ref):
    k = pl.program_id(1)

    @pl.when(k == 0)
    def _():
        acc_ref[...] = jnp.zeros_like(acc_ref)

    xb = xb_ref[...]
    hg = jnp.dot(xb, wg_ref[...], preferred_element_type=F32)
    hu = jnp.dot(xb, wu_ref[...], preferred_element_type=F32)
    h = (hg * jax.nn.sigmoid(hg) * hu).astype(BF16)
    for n in range(D_MODEL // FFN_TN):
        cols = slice(n * FFN_TN, (n + 1) * FFN_TN)
        acc_ref[:, cols] += jnp.dot(h, wd_ref[:, cols], preferred_element_type=F32)

    @pl.when(k == pl.num_programs(1) - 1)
    def _():
        y = ALPHA * x_ref[...] + acc_ref[...]
        x2 = _layernorm_rows(y, g_ref[...], b_ref[...])
        x2_ref[...] = x2
        x2b_ref[...] = x2.astype(BF16)


def _ffn_call(x1b, wg, wu, wd, x1, g, b):
    row_spec = pl.BlockSpec((FFN_TM, D_MODEL), lambda i, k: (i, 0))
    vec_spec = pl.BlockSpec((1, D_MODEL), lambda i, k: (0, 0))
    return pl.pallas_call(
        _ffn_kernel,
        grid=(T_ALL // FFN_TM, D_FF // FFN_TF),
        in_specs=[row_spec,
                  pl.BlockSpec((D_MODEL, FFN_TF), lambda i, k: (0, k)),
                  pl.BlockSpec((D_MODEL, FFN_TF), lambda i, k: (0, k)),
                  pl.BlockSpec((FFN_TF, D_MODEL), lambda i, k: (k, 0)),
                  row_spec, vec_spec, vec_spec],
        out_specs=[row_spec, row_spec],
        out_shape=[jax.ShapeDtypeStruct((T_ALL, D_MODEL), F32),
                   jax.ShapeDtypeStruct((T_ALL, D_MODEL), BF16)],
        scratch_shapes=[pltpu.VMEM((FFN_TM, D_MODEL), F32)],
        compiler_params=pltpu.CompilerParams(
            dimension_semantics=("arbitrary", "arbitrary"),
            vmem_limit_bytes=48 * MIB),
        name="ffn",
    )(x1b, wg, wu, wd, x1, g, b)


def _row(v):
    return v.reshape(1, -1)


def _sublane_rows(w):
    return jnp.broadcast_to(w[:, None, :], (w.shape[0], SUBLANES, w.shape[1]))


def kernel(x_prompt, x_sample, state_conv_a, state_conv_b, w_in, b_gate, conv_a_w, conv_b_w,
           conv_b_b, ln_b_g, ln_b_b, ln_v_g, ln_v_b, w_s, b_s, w_a_out, w_b_out, w_c_out,
           w_o, ln1_g, ln1_b, w_gate, w_up, w_down, ln2_g, ln2_b):
    x = jnp.concatenate([x_prompt.reshape(SEQ, D_MODEL), x_sample.reshape(T_SAMPLE, D_MODEL)], axis=0)
    xb = x.astype(BF16)
    reps = CHUNK_C // DEC_SEQ
    cast_weights = (w_a_out, w_b_out, w_c_out, w_o, w_gate, w_up, w_down)
    pa, pb, sa, sb, sv = [], [], [], [], []
    for l in range(DEPTH):
        bgate = b_gate[l].reshape(N_BRANCH, D_MODEL)
        halo_a = jnp.pad(state_conv_a[l], ((0, 0), (HALO_A - (CONV_A - 1), 0), (0, 0)))
        halo_b = jnp.pad(state_conv_b[l], ((0, 0), (HALO_B - (CONV_B - 1), 0), (0, 0)))
        ws_stack = jnp.stack([w_s[l], jnp.tile(w_s[l][:, :DEC_SEQ, :DEC_SEQ], (1, reps, reps))])
        bs_rows = jnp.repeat(b_s[l].T, GROUP_C, axis=1)
        bs_stack = jnp.stack([bs_rows, jnp.tile(bs_rows[:DEC_SEQ], (reps, 1))])

        (ca, bg, ab, gu, gv, g0, g1, g2), (wa_b, wb_b, wc_b, wo_b, wg_b, wu_b, wd_b) = _proj_call(
            l, xb, w_in, bgate, cast_weights)
        za, zb, zc, vn = _mix_call(ca, bg, ab, gu, gv, halo_a, halo_b,
                                   _sublane_rows(conv_a_w[l]), _sublane_rows(conv_b_w[l]),
                                   _row(conv_b_b[l]), _row(ln_b_g[l]), _row(ln_b_b[l]),
                                   _row(ln_v_g[l]), _row(ln_v_b[l]), ws_stack, bs_stack)
        m = _merge_call(za, zb, zc, wa_b, wb_b, wc_b, g0, g1, g2)
        x1, x1b = _outp_call(m, wo_b, x, _row(ln1_g[l]), _row(ln1_b[l]))
        x, xb = _ffn_call(x1b, wg_b, wu_b, wd_b, x1, _row(ln2_g[l]), _row(ln2_b[l]))

        pa.append(ca[SEQ - (CONV_A - 1):SEQ][None])
        pb.append(ab[SEQ - (CONV_B - 1):SEQ][None])
        ca_s = ca[SEQ:].reshape(DEC_BATCH, DEC_SEQ, D_A)
        ab_s = ab[SEQ:].reshape(DEC_BATCH, DEC_SEQ, D_B)
        sa.append(ca_s[:, DEC_SEQ - (CONV_A - 1):])
        sb.append(ab_s[:, DEC_SEQ - (CONV_B - 1):])
        sv.append(vn[SEQ:].reshape(DEC_BATCH, DEC_SEQ, D_C))
    y_prompt = x[:SEQ].reshape(1, SEQ, D_MODEL)
    y_sample = x[SEQ:].reshape(DEC_BATCH, DEC_SEQ, D_MODEL)
    return (y_prompt, y_sample, jnp.stack(pa), jnp.stack(pb), jnp.stack(sa), jnp.stack(sb),
            jnp.stack(sv))
```

```python
import functools

import jax
import jax.numpy as jnp
from jax import lax
from jax.experimental import pallas as pl
from jax.experimental.pallas import tpu as pltpu

D_MODEL = 2048
SEQ = 8192
DEPTH = 2
DEC_BATCH = 32
DEC_SEQ = 32
D_A = 2048
CONV_A = 3
D_B = 2048
CONV_B = 31
D_C = 2048
CHUNK_C = 128
N_GROUPS_C = 16
GROUP_C = D_C // N_GROUPS_C
N_BRANCH = 3
D_FF = 5632
ALPHA = (2 * DEPTH) ** 0.25
LN_EPS = 1e-5
N_PROJ_GROUPS = 7 + N_BRANCH

T_SAMPLE = DEC_BATCH * DEC_SEQ
T_ALL = SEQ + T_SAMPLE

SUBLANES = 8
LANES = 128
MIB = 1024 * 1024

F32 = jnp.float32
BF16 = jnp.bfloat16

PROJ_TM = T_SAMPLE
PROJ_TC = LANES
CAST_STEPS = 8
CAST_COL_SPLIT = 4
CAST_ROW_SPLIT = CAST_STEPS // CAST_COL_SPLIT
SUB = DEC_SEQ
HALO_A = SUBLANES
HALO_B = SUB
MIX_TM = 256
MIX_ROWS = 32
MERGE_TM = 512
MERGE_TN = 1024
OUT_TM = 512
OUT_ROWS = 256
FFN_TM = 512
FFN_TF = 512
FFN_TN = 512

N_PROMPT_MIX_TILES = SEQ // MIX_TM
N_PROMPT_FFN_TILES = SEQ // FFN_TM


def _layernorm_rows(x, g, b):
    mu = jnp.mean(x, axis=-1, keepdims=True)
    xc = x - mu
    var = jnp.mean(xc * xc, axis=-1, keepdims=True)
    return xc * lax.rsqrt(var + LN_EPS) * g + b


def _causal_taps(blocks, w_ref, n_taps, halo):
    first = halo - (n_taps - 1)
    y = None
    for rho in range(SUBLANES):
        taps = [k for k in range(n_taps) if (first + k) % SUBLANES == rho]
        if not taps:
            continue
        n_out = SUB // SUBLANES + (1 if rho else 0)
        acc = [None] * n_out
        for k in taps:
            q = (first + k) // SUBLANES
            w8 = w_ref[k]
            for b in range(n_out):
                term = w8 * blocks[q + b]
                acc[b] = term if acc[b] is None else acc[b] + term
        part = jnp.concatenate(acc, axis=0)
        if rho:
            part = part[rho:rho + SUB]
        y = part if y is None else y + part
    return y


N_CAST = 7


def _proj_kernel(*refs):
    x_ref = refs[0]
    w_refs = refs[1:1 + N_PROJ_GROUPS]
    (bgate_ref, wa_ref, wb_ref, bb_ref, ha_ref, hb_ref) = refs[1 + N_PROJ_GROUPS:7 + N_PROJ_GROUPS]
    cast_in = refs[7 + N_PROJ_GROUPS:7 + N_PROJ_GROUPS + N_CAST]
    outs = refs[7 + N_PROJ_GROUPS + N_CAST:]
    ca_ref, ab_ref, za_ref, yb_ref, gu_ref, gv_ref, g0_ref, g1_ref, g2_ref = outs[:9]
    cast_out = outs[9:9 + N_CAST]
    wsc_ref, bg_ref, carry_a_ref, carry_b_ref = outs[9 + N_CAST:]
    tc = PROJ_TC
    i = pl.program_id(1)
    is_sample = i == pl.num_programs(1) - 1

    @pl.when(i == 0)
    def _():
        for g in range(N_PROJ_GROUPS):
            wsc_ref[:, g * tc:(g + 1) * tc] = w_refs[g][0].astype(BF16)
        carry_a_ref[...] = jnp.zeros_like(carry_a_ref)
        carry_b_ref[...] = jnp.zeros_like(carry_b_ref)

    @pl.when(i < CAST_STEPS)
    def _():
        for src, dst in zip(cast_in, cast_out):
            dst[...] = src[0].astype(BF16)

    x = x_ref[...]

    def pair(p):
        return jnp.dot(x, wsc_ref[:, 2 * p * tc:(2 * p + 2) * tc], preferred_element_type=F32)

    r2 = pair(2)
    gu_ref[...] = jax.nn.gelu(r2[:, tc:])
    sg = jax.nn.sigmoid(r2[:, :tc])
    r1 = pair(1)
    ab_ref[...] = r1[:, tc:] * sg
    r0 = pair(0)
    ca_ref[...] = r1[:, :tc] * r0[:, :tc]
    bg_ref[...] = r0[:, tc:]
    r3 = pair(3)
    gv_ref[...] = jax.nn.gelu(r3[:, :tc])
    g0_ref[...] = jax.nn.sigmoid(r3[:, tc:] + bgate_ref[0:1, :])
    r4 = pair(4)
    g1_ref[...] = jax.nn.sigmoid(r4[:, :tc] + bgate_ref[1:2, :])
    g2_ref[...] = jax.nn.sigmoid(r4[:, tc:] + bgate_ref[2:3, :])

    def row_blocks(ref, lo, n):
        return [ref[lo + b * SUBLANES:lo + (b + 1) * SUBLANES, :] for b in range(n)]

    n_cur = SUB // SUBLANES
    n_halo_b = HALO_B // SUBLANES
    for r in range(PROJ_TM // SUB):
        lo = r * SUB
        if r == 0:
            above_a = [carry_a_ref[...]]
            above_b = row_blocks(carry_b_ref, 0, n_halo_b)
        else:
            above_a = row_blocks(ca_ref, lo - HALO_A, 1)
            above_b = row_blocks(ab_ref, lo - HALO_B, n_halo_b)
        state_a = [ha_ref[r]]
        state_b = [hb_ref[r, b * SUBLANES:(b + 1) * SUBLANES, :] for b in range(n_halo_b)]
        halo_a = [jnp.where(is_sample, s, p) for s, p in zip(state_a, above_a)]
        halo_b = [jnp.where(is_sample, s, p) for s, p in zip(state_b, above_b)]
        ya = _causal_taps(halo_a + row_blocks(ca_ref, lo, n_cur), wa_ref, CONV_A, HALO_A)
        yb = _causal_taps(halo_b + row_blocks(ab_ref, lo, n_cur), wb_ref, CONV_B, HALO_B)
        za_ref[lo:lo + SUB, :] = (bg_ref[lo:lo + SUB, :] * ya).astype(BF16)
        yb_ref[lo:lo + SUB, :] = yb + bb_ref[...]

    carry_a_ref[...] = ca_ref[PROJ_TM - HALO_A:, :]
    carry_b_ref[...] = ab_ref[PROJ_TM - HALO_B:, :]


def _proj_call(l, xb, w_in, bgate, conv_a_w8, conv_b_w8, conv_b_b, halo_a, halo_b, cast_weights):
    n_c = D_MODEL // PROJ_TC
    n_m = T_ALL // PROJ_TM

    def cast_row(j, i):
        return CAST_ROW_SPLIT * j + jnp.minimum(i, CAST_STEPS - 1) // CAST_COL_SPLIT

    def cast_col(i):
        return jnp.minimum(i, CAST_STEPS - 1) % CAST_COL_SPLIT

    w_specs = [pl.BlockSpec((1, D_MODEL, PROJ_TC), lambda j, i, g=g: (l, 0, g * n_c + j))
               for g in range(N_PROJ_GROUPS)]
    cast_in_specs, cast_out_specs, cast_out_shapes = [], [], []
    for w in cast_weights:
        rows, cols = w.shape[1] // (n_c * CAST_ROW_SPLIT), w.shape[2] // CAST_COL_SPLIT
        cast_in_specs.append(pl.BlockSpec((1, rows, cols), lambda j, i: (l, cast_row(j, i), cast_col(i))))
        cast_out_specs.append(pl.BlockSpec((rows, cols), lambda j, i: (cast_row(j, i), cast_col(i))))
        cast_out_shapes.append(jax.ShapeDtypeStruct(w.shape[1:], BF16))
    tile_spec = pl.BlockSpec((PROJ_TM, PROJ_TC), lambda j, i: (i, j))
    f32_out = jax.ShapeDtypeStruct((T_ALL, D_MODEL), F32)
    bf16_out = jax.ShapeDtypeStruct((T_ALL, D_MODEL), BF16)
    res = pl.pallas_call(
        _proj_kernel,
        grid=(n_c, n_m),
        in_specs=[pl.BlockSpec((PROJ_TM, D_MODEL), lambda j, i: (i, 0))] + w_specs + [
            pl.BlockSpec((N_BRANCH, PROJ_TC), lambda j, i: (0, j)),
            pl.BlockSpec((CONV_A, SUBLANES, PROJ_TC), lambda j, i: (0, 0, j)),
            pl.BlockSpec((CONV_B, SUBLANES, PROJ_TC), lambda j, i: (0, 0, j)),
            pl.BlockSpec((1, PROJ_TC), lambda j, i: (0, j)),
            pl.BlockSpec((DEC_BATCH, HALO_A, PROJ_TC), lambda j, i: (0, 0, j)),
            pl.BlockSpec((DEC_BATCH, HALO_B, PROJ_TC), lambda j, i: (0, 0, j)),
        ] + cast_in_specs,
        out_specs=[tile_spec] * 9 + cast_out_specs,
        out_shape=[f32_out, f32_out, bf16_out] + [f32_out] * 6 + cast_out_shapes,
        scratch_shapes=[pltpu.VMEM((D_MODEL, N_PROJ_GROUPS * PROJ_TC), BF16),
                        pltpu.VMEM((PROJ_TM, PROJ_TC), F32),
                        pltpu.VMEM((HALO_A, PROJ_TC), F32),
                        pltpu.VMEM((HALO_B, PROJ_TC), F32)],
        compiler_params=pltpu.CompilerParams(
            dimension_semantics=("arbitrary", "arbitrary"),
            vmem_limit_bytes=58 * MIB),
        name="proj",
    )(xb, *([w_in] * N_PROJ_GROUPS), bgate, conv_a_w8, conv_b_w8, conv_b_b, halo_a, halo_b,
      *cast_weights)
    return res[:9], res[9:]


def _mix_kernel(yb_ref, gu_ref, gv_ref, lnbg_ref, lnbb_ref, lnvg_ref, lnvb_ref, ws_ref, bs_ref,
                zb_ref, zc_ref, vn_ref):
    i = pl.program_id(0)
    is_prompt = i < N_PROMPT_MIX_TILES

    def row_step(q, carry):
        rows = pl.ds(pl.multiple_of(q * MIX_ROWS, MIX_ROWS), MIX_ROWS)
        nb = _layernorm_rows(yb_ref[rows, :], lnbg_ref[...], lnbb_ref[...])
        zb_ref[rows, :] = (nb * jax.nn.sigmoid(nb)).astype(BF16)
        vn_ref[rows, :] = _layernorm_rows(gv_ref[rows, :], lnvg_ref[...], lnvb_ref[...])
        return carry

    lax.fori_loop(0, MIX_TM // MIX_ROWS, row_step, 0)

    ri = lax.broadcasted_iota(jnp.int32, (CHUNK_C, CHUNK_C), 0)
    ci = lax.broadcasted_iota(jnp.int32, (CHUNK_C, CHUNK_C), 1)
    span_log2 = jnp.where(is_prompt, CHUNK_C.bit_length() - 1, DEC_SEQ.bit_length() - 1)
    span_low = jnp.where(is_prompt, CHUNK_C - 1, DEC_SEQ - 1)
    mask = jnp.logical_and((ri >> span_log2) == (ci >> span_log2),
                           (ci & span_low) <= (ri & span_low))
    for g in range(N_GROUPS_C):
        lanes = slice(g * GROUP_C, (g + 1) * GROUP_C)
        wm = jnp.where(mask, ws_ref[0, g], 0.0).astype(BF16)
        for q in range(MIX_TM // CHUNK_C):
            rows = slice(q * CHUNK_C, (q + 1) * CHUNK_C)
            s = jnp.dot(wm, vn_ref[rows, lanes].astype(BF16), preferred_element_type=F32)
            s = s + bs_ref[0, :, lanes]
            zc_ref[rows, lanes] = (gu_ref[rows, lanes] * s).astype(BF16)


def _mix_call(yb, gu, gv, ln_b_g, ln_b_b, ln_v_g, ln_v_b, ws_stack, bs_stack):
    row_spec = pl.BlockSpec((MIX_TM, D_MODEL), lambda i: (i, 0))

    def which(i):
        return jnp.where(i < N_PROMPT_MIX_TILES, 0, 1)

    vec_spec = pl.BlockSpec((1, D_MODEL), lambda i: (0, 0))
    return pl.pallas_call(
        _mix_kernel,
        grid=(T_ALL // MIX_TM,),
        in_specs=[
            row_spec, row_spec, row_spec, vec_spec, vec_spec, vec_spec, vec_spec,
            pl.BlockSpec((1, N_GROUPS_C, CHUNK_C, CHUNK_C), lambda i: (which(i), 0, 0, 0)),
            pl.BlockSpec((1, CHUNK_C, D_MODEL), lambda i: (which(i), 0, 0)),
        ],
        out_specs=[row_spec, row_spec, row_spec],
        out_shape=[
            jax.ShapeDtypeStruct((T_ALL, D_MODEL), BF16),
            jax.ShapeDtypeStruct((T_ALL, D_MODEL), BF16),
            jax.ShapeDtypeStruct((T_ALL, D_MODEL), F32),
        ],
        compiler_params=pltpu.CompilerParams(
            dimension_semantics=("arbitrary",),
            vmem_limit_bytes=40 * MIB),
        name="mix",
    )(yb, gu, gv, ln_b_g, ln_b_b, ln_v_g, ln_v_b, ws_stack, bs_stack)


def _merge_kernel(za_ref, zb_ref, zc_ref, wa_ref, wb_ref, wc_ref,
                  g0_ref, g1_ref, g2_ref, m_ref):
    pa = jnp.dot(za_ref[...], wa_ref[...], preferred_element_type=F32)
    m = g0_ref[...] * pa
    pb = jnp.dot(zb_ref[...], wb_ref[...], preferred_element_type=F32)
    m = m + g1_ref[...] * pb
    pc = jnp.dot(zc_ref[...], wc_ref[...], preferred_element_type=F32)
    m = m + g2_ref[...] * pc
    m_ref[...] = m.astype(BF16)


def _merge_call(za, zb, zc, wa, wb, wc, g0, g1, g2):
    n_n = D_MODEL // MERGE_TN
    n_m = T_ALL // MERGE_TM
    z_spec = pl.BlockSpec((MERGE_TM, D_MODEL), lambda j, i: (i, 0))
    w_spec = pl.BlockSpec((D_MODEL, MERGE_TN), lambda j, i: (0, j))
    t_spec = pl.BlockSpec((MERGE_TM, MERGE_TN), lambda j, i: (i, j))
    return pl.pallas_call(
        _merge_kernel,
        grid=(n_n, n_m),
        in_specs=[z_spec, z_spec, z_spec, w_spec, w_spec, w_spec, t_spec, t_spec, t_spec],
        out_specs=t_spec,
        out_shape=jax.ShapeDtypeStruct((T_ALL, D_MODEL), BF16),
        compiler_params=pltpu.CompilerParams(
            dimension_semantics=("arbitrary", "arbitrary"),
            vmem_limit_bytes=56 * MIB),
        name="merge",
    )(za, zb, zc, wa, wb, wc, g0, g1, g2)


def _outp_kernel(m_ref, wo_ref, x_ref, g_ref, b_ref, x1_ref, x1b_ref):
    for c in range(OUT_TM // OUT_ROWS):
        rows = slice(c * OUT_ROWS, (c + 1) * OUT_ROWS)
        y = ALPHA * x_ref[rows, :] + jnp.dot(m_ref[rows, :], wo_ref[...], preferred_element_type=F32)
        x1 = _layernorm_rows(y, g_ref[...], b_ref[...])
        x1_ref[rows, :] = x1
        x1b_ref[rows, :] = x1.astype(BF16)


def _outp_call(m, wo, x, g, b):
    row_spec = pl.BlockSpec((OUT_TM, D_MODEL), lambda i: (i, 0))
    vec_spec = pl.BlockSpec((1, D_MODEL), lambda i: (0, 0))
    return pl.pallas_call(
        _outp_kernel,
        grid=(T_ALL // OUT_TM,),
        in_specs=[row_spec, pl.BlockSpec((D_MODEL, D_MODEL), lambda i: (0, 0)), row_spec,
                  vec_spec, vec_spec],
        out_specs=[row_spec, row_spec],
        out_shape=[jax.ShapeDtypeStruct((T_ALL, D_MODEL), F32),
                   jax.ShapeDtypeStruct((T_ALL, D_MODEL), BF16)],
        compiler_params=pltpu.CompilerParams(
            dimension_semantics=("arbitrary",),
            vmem_limit_bytes=48 * MIB),
        name="outp",
    )(m, wo, x, g, b)


def _ffn_kernel(last_layer, xb_ref, wg_ref, wu_ref, wd_ref, x_ref, g_ref, b_ref,
                out0_ref, out1_ref, acc_ref):
    i = pl.program_id(0)
    k = pl.program_id(1)

    @pl.when(k == 0)
    def _():
        acc_ref[...] = jnp.zeros_like(acc_ref)

    xb = xb_ref[...]
    hg = jnp.dot(xb, wg_ref[...], preferred_element_type=F32)
    hu = jnp.dot(xb, wu_ref[...], preferred_element_type=F32)
    h = (hg * jax.nn.sigmoid(hg) * hu).astype(BF16)
    for n in range(D_MODEL // FFN_TN):
        cols = slice(n * FFN_TN, (n + 1) * FFN_TN)
        acc_ref[:, cols] += jnp.dot(h, wd_ref[:, cols], preferred_element_type=F32)

    @pl.when(k == pl.num_programs(1) - 1)
    def _():
        y = ALPHA * x_ref[...] + acc_ref[...]
        x2 = _layernorm_rows(y, g_ref[...], b_ref[...])
        if last_layer:
            @pl.when(i < N_PROMPT_FFN_TILES)
            def _():
                out0_ref[...] = x2

            @pl.when(i >= N_PROMPT_FFN_TILES)
            def _():
                out1_ref[...] = x2
        else:
            out0_ref[...] = x2
            out1_ref[...] = x2.astype(BF16)


def _ffn_call(x1b, wg, wu, wd, x1, g, b, last_layer):
    row_spec = pl.BlockSpec((FFN_TM, D_MODEL), lambda i, k: (i, 0))
    vec_spec = pl.BlockSpec((1, D_MODEL), lambda i, k: (0, 0))
    if last_layer:
        out_specs = [
            pl.BlockSpec((FFN_TM, D_MODEL), lambda i, k: (jnp.minimum(i, N_PROMPT_FFN_TILES - 1), 0)),
            pl.BlockSpec((FFN_TM, D_MODEL), lambda i, k: (jnp.maximum(i - N_PROMPT_FFN_TILES, 0), 0)),
        ]
        out_shape = [jax.ShapeDtypeStruct((SEQ, D_MODEL), F32),
                     jax.ShapeDtypeStruct((T_SAMPLE, D_MODEL), F32)]
    else:
        out_specs = [row_spec, row_spec]
        out_shape = [jax.ShapeDtypeStruct((T_ALL, D_MODEL), F32),
                     jax.ShapeDtypeStruct((T_ALL, D_MODEL), BF16)]
    return pl.pallas_call(
        functools.partial(_ffn_kernel, last_layer),
        grid=(T_ALL // FFN_TM, D_FF // FFN_TF),
        in_specs=[row_spec,
                  pl.BlockSpec((D_MODEL, FFN_TF), lambda i, k: (0, k)),
                  pl.BlockSpec((D_MODEL, FFN_TF), lambda i, k: (0, k)),
                  pl.BlockSpec((FFN_TF, D_MODEL), lambda i, k: (k, 0)),
                  row_spec, vec_spec, vec_spec],
        out_specs=out_specs,
        out_shape=out_shape,
        scratch_shapes=[pltpu.VMEM((FFN_TM, D_MODEL), F32)],
        compiler_params=pltpu.CompilerParams(
            dimension_semantics=("arbitrary", "arbitrary"),
            vmem_limit_bytes=56 * MIB),
        name="ffn",
    )(x1b, wg, wu, wd, x1, g, b)


def _row(v):
    return v.reshape(1, -1)


def _sublane_rows(w):
    return jnp.broadcast_to(w[:, None, :], (w.shape[0], SUBLANES, w.shape[1]))


def kernel(x_prompt, x_sample, state_conv_a, state_conv_b, w_in, b_gate, conv_a_w, conv_b_w,
           conv_b_b, ln_b_g, ln_b_b, ln_v_g, ln_v_b, w_s, b_s, w_a_out, w_b_out, w_c_out,
           w_o, ln1_g, ln1_b, w_gate, w_up, w_down, ln2_g, ln2_b):
    x = jnp.concatenate([x_prompt.reshape(SEQ, D_MODEL), x_sample.reshape(T_SAMPLE, D_MODEL)], axis=0)
    xb = x.astype(BF16)
    reps = CHUNK_C // DEC_SEQ
    cast_weights = (w_a_out, w_b_out, w_c_out, w_o, w_gate, w_up, w_down)
    pa, pb, sa, sb, sv = [], [], [], [], []
    for l in range(DEPTH):
        bgate = b_gate[l].reshape(N_BRANCH, D_MODEL)
        halo_a = jnp.pad(state_conv_a[l], ((0, 0), (HALO_A - (CONV_A - 1), 0), (0, 0)))
        halo_b = jnp.pad(state_conv_b[l], ((0, 0), (HALO_B - (CONV_B - 1), 0), (0, 0)))
        ws_stack = jnp.stack([w_s[l], jnp.tile(w_s[l][:, :DEC_SEQ, :DEC_SEQ], (1, reps, reps))])
        bs_rows = jnp.repeat(b_s[l].T, GROUP_C, axis=1)
        bs_stack = jnp.stack([bs_rows, jnp.tile(bs_rows[:DEC_SEQ], (reps, 1))])

        (ca, ab, za, yb, gu, gv, g0, g1, g2), (wa_b, wb_b, wc_b, wo_b, wg_b, wu_b, wd_b) = _proj_call(
            l, xb, w_in, bgate, _sublane_rows(conv_a_w[l]), _sublane_rows(conv_b_w[l]),
            _row(conv_b_b[l]), halo_a, halo_b, cast_weights)
        zb, zc, vn = _mix_call(yb, gu, gv, _row(ln_b_g[l]), _row(ln_b_b[l]),
                               _row(ln_v_g[l]), _row(ln_v_b[l]), ws_stack, bs_stack)
        m = _merge_call(za, zb, zc, wa_b, wb_b, wc_b, g0, g1, g2)
        x1, x1b = _outp_call(m, wo_b, x, _row(ln1_g[l]), _row(ln1_b[l]))
        x, xb = _ffn_call(x1b, wg_b, wu_b, wd_b, x1, _row(ln2_g[l]), _row(ln2_b[l]),
                          last_layer=(l == DEPTH - 1))

        pa.append(ca[SEQ - (CONV_A - 1):SEQ][None])
        pb.append(ab[SEQ - (CONV_B - 1):SEQ][None])
        ca_s = ca[SEQ:].reshape(DEC_BATCH, DEC_SEQ, D_A)
        ab_s = ab[SEQ:].reshape(DEC_BATCH, DEC_SEQ, D_B)
        sa.append(ca_s[:, DEC_SEQ - (CONV_A - 1):])
        sb.append(ab_s[:, DEC_SEQ - (CONV_B - 1):])
        sv.append(vn[SEQ:].reshape(DEC_BATCH, DEC_SEQ, D_C))
    y_prompt = x.reshape(1, SEQ, D_MODEL)
    y_sample = xb.reshape(DEC_BATCH, DEC_SEQ, D_MODEL)
    return (y_prompt, y_sample, jnp.stack(pa), jnp.stack(pb), jnp.stack(sa), jnp.stack(sb),
            jnp.stack(sv))
```
